```python
import functools
import jax, jax.numpy as jnp
from jax import lax
import numpy as np

D_MODEL = 1024
BATCH = 8
SEQ = 2048
DEPTH = 2
DEC_BATCH = 32
DEC_SEQ = 4
PAST_LEN = 16384
PAGE_SIZE = 128

ATT_HEADS = 8
HEAD_DIM = 64
ATT_WIDTH = ATT_HEADS * HEAD_DIM
CONV_WIDTH = D_MODEL // 4
CONV_K = 3
POOL_WINDOWS = (2, 4, 8, 16)
POOL_GROUPS = len(POOL_WINDOWS)
POOL_WIDTH = D_MODEL // 4
POOL_GROUP_DIM = POOL_WIDTH // POOL_GROUPS
POOL_HIST = max(POOL_WINDOWS) - 1
MIX_WIDTH = ATT_WIDTH + CONV_WIDTH + POOL_WIDTH
D_FF = -(-8 * D_MODEL // (3 * 256)) * 256
PLE_DIM = 256
Q_BLOCK = 128
RMS_EPS = 1e-6
FORGET_BIAS_INIT = 7.0
FORGET_W_SCALE = 0.1
Q_OFF = 0
K_OFF = Q_OFF + ATT_WIDTH
V_OFF = K_OFF + ATT_WIDTH
F_OFF = V_OFF + ATT_WIDTH
CB_OFF = F_OFF + ATT_HEADS
CC_OFF = CB_OFF + CONV_WIDTH
CH_OFF = CC_OFF + CONV_WIDTH
PU_OFF = CH_OFF + CONV_WIDTH
PROJ_WIDTH = PU_OFF + POOL_WIDTH

kernel_name = 'hymba_fox_conv_pool_decoder_step'


def rmsnorm(x, g):
    xf = x.astype(jnp.float32)
    y = xf * lax.rsqrt(jnp.mean(xf * xf, axis=-1, keepdims=True) + RMS_EPS)
    return (y * g.astype(jnp.float32)).astype(x.dtype)


def fox_prompt(q, k, v, logf):
    b, s, h, d = q.shape
    n_blk = s // Q_BLOCK
    scale = HEAD_DIM ** -0.5
    neg = jnp.finfo(jnp.float32).min
    cum = jnp.cumsum(logf, axis=1).transpose(0, 2, 1)
    q_blk = q.reshape(b, n_blk, Q_BLOCK, h, d).swapaxes(0, 1)
    c_blk = cum.reshape(b, h, n_blk, Q_BLOCK).transpose(2, 0, 1, 3)
    k_pos = jnp.arange(s)

    def one_block(args):
        qi, ci, i = args
        sc = jnp.einsum('bqhd,bkhd->bhqk', qi, k).astype(jnp.float32) * scale
        sc = sc + ci[..., :, None] - cum[:, :, None, :]
        q_pos = i * Q_BLOCK + jnp.arange(Q_BLOCK)
        sc = jnp.where(q_pos[:, None] >= k_pos[None, :], sc, neg)
        p = jax.nn.softmax(sc, axis=-1).astype(v.dtype)
        return jnp.einsum('bhqk,bkhd->bqhd', p, v)

    out = lax.map(one_block, (q_blk, c_blk, jnp.arange(n_blk)))
    return out.swapaxes(0, 1).reshape(b, s, h * d)


def fox_sample(q, k, v, logf, k_past, v_past, logf_past):
    b, t, h, d = q.shape
    p_len = k_past.shape[1]
    scale = HEAD_DIM ** -0.5
    neg = jnp.finfo(jnp.float32).min
    cum = jnp.cumsum(logf, axis=1).transpose(0, 2, 1)
    lp = logf_past.astype(jnp.float32)
    tail = (lax.cumsum(lp, axis=1, reverse=True) - lp).transpose(0, 2, 1)
    s_past = jnp.einsum('bqhd,bkhd->bhqk', q, k_past).astype(jnp.float32) * scale
    s_past = s_past + cum[..., :, None] + tail[:, :, None, :]
    s_new = jnp.einsum('bqhd,bkhd->bhqk', q, k).astype(jnp.float32) * scale
    s_new = s_new + cum[..., :, None] - cum[:, :, None, :]
    s_new = jnp.where(jnp.tril(jnp.ones((t, t), dtype=bool)), s_new, neg)
    p = jax.nn.softmax(jnp.concatenate([s_past, s_new], axis=-1), axis=-1)
    out = (jnp.einsum('bhqk,bkhd->bqhd', p[..., :p_len].astype(v.dtype), v_past)
           + jnp.einsum('bhqk,bkhd->bqhd', p[..., p_len:].astype(v.dtype), v))
    return out.reshape(b, t, h * d)


def short_conv_mixer(b_gate, c_gate, h_in, hist, conv_w):
    t = h_in.shape[1]
    full = jnp.concatenate([hist.astype(h_in.dtype), c_gate * h_in], axis=1)
    out = sum(conv_w[j] * full[:, j:j + t] for j in range(CONV_K))
    return b_gate * out, full[:, t:]


def pool_mixer(u, hist, pos0, pool_w, pool_scale):
    b, t, _ = u.shape
    full = jnp.concatenate([hist.astype(u.dtype), u], axis=1)
    cs = jnp.cumsum(full.astype(jnp.float32), axis=1)
    cs = jnp.concatenate([jnp.zeros((b, 1, POOL_WIDTH), jnp.float32), cs], axis=1)
    pos = pos0 + jnp.arange(t)
    end = POOL_HIST + 1
    outs = []
    for g, w in enumerate(POOL_WINDOWS):
        sl = slice(g * POOL_GROUP_DIM, (g + 1) * POOL_GROUP_DIM)
        win = cs[:, end:end + t, sl] - cs[:, end - w:end - w + t, sl]
        cnt = jnp.minimum(pos + 1, w).astype(jnp.float32)[None, :, None]
        diff = (win / cnt).astype(u.dtype) - u[:, :, sl]
        outs.append(diff @ pool_w[g])
    y = jnp.concatenate(outs, axis=-1) * pool_scale
    return y, full[:, t:]


def trunk_layer(x, ple, wl, attend, conv_hist, pool_hist, pos0):
    b, t, _ = x.shape
    h = rmsnorm(x, wl['g_pre_mix'])
    z = h @ wl['w_in']
    q = z[..., Q_OFF:K_OFF].reshape(b, t, ATT_HEADS, HEAD_DIM)
    k = z[..., K_OFF:V_OFF].reshape(b, t, ATT_HEADS, HEAD_DIM)
    v = z[..., V_OFF:F_OFF].reshape(b, t, ATT_HEADS, HEAD_DIM)
    logf = jax.nn.log_sigmoid(z[..., F_OFF:CB_OFF].astype(jnp.float32) + wl['b_f'].astype(jnp.float32))
    att = attend(q, k, v, logf)
    conv, conv_new = short_conv_mixer(z[..., CB_OFF:CC_OFF], z[..., CC_OFF:CH_OFF],
                                      z[..., CH_OFF:PU_OFF], conv_hist, wl['conv_w'])
    pool, pool_new = pool_mixer(z[..., PU_OFF:PROJ_WIDTH], pool_hist, pos0, wl['pool_w'], wl['pool_scale'])
    mix = jnp.concatenate([rmsnorm(att, wl['g_att']), rmsnorm(conv, wl['g_conv']),
                           rmsnorm(pool, wl['g_pool'])], axis=-1)
    x = x + rmsnorm(mix @ wl['w_o'], wl['g_post_mix'])
    h = rmsnorm(x, wl['g_pre_ffn'])
    gu = h @ wl['w_gu']
    f = (jax.nn.silu(gu[..., :D_FF]) * gu[..., D_FF:]) @ wl['w_down']
    x = x + rmsnorm(f, wl['g_post_ffn'])
    x = x + (ple @ wl['w_ple_proj']) * jax.nn.sigmoid(x @ wl['w_ple_gate'])
    return x, (k, v, logf, conv_new, pool_new)


def setup_inputs(seed: int = 0) -> dict:
    key = jax.random.key(seed)
    ks = jax.random.split(key, 32)
    n_pages = PAST_LEN // PAGE_SIZE
    n_used = DEC_BATCH * n_pages
    n_pool = n_used + n_used // 4

    def nrm(k, shape, s=1.0):
        return s * jax.random.normal(k, shape, jnp.float32)

    def gain(k, n):
        return 1.0 + nrm(k, (DEPTH, n), 0.02)

    perm = jax.random.permutation(ks[9], n_pool)
    page_table = perm[:n_used].reshape(DEC_BATCH, n_pages).astype(jnp.int32)
    col_scale = jnp.ones((PROJ_WIDTH,), jnp.float32).at[F_OFF:CB_OFF].set(FORGET_W_SCALE)
    return {
        'x_prompt': nrm(ks[0], (BATCH, SEQ, D_MODEL)),
        'x_sample': nrm(ks[1], (DEC_BATCH, DEC_SEQ, D_MODEL)),
        'p_prompt': nrm(ks[2], (DEPTH, BATCH, SEQ, PLE_DIM)),
        'p_sample': nrm(ks[3], (DEPTH, DEC_BATCH, DEC_SEQ, PLE_DIM)),
        'cache_k': nrm(ks[4], (DEPTH, n_pool, PAGE_SIZE, ATT_HEADS, HEAD_DIM)),
        'cache_v': nrm(ks[5], (DEPTH, n_pool, PAGE_SIZE, ATT_HEADS, HEAD_DIM)),
        'cache_logf': jax.nn.log_sigmoid(FORGET_BIAS_INIT + nrm(ks[6], (DEPTH, n_pool, PAGE_SIZE, ATT_HEADS), 0.5)),
        'state_conv': nrm(ks[7], (DEPTH, DEC_BATCH, CONV_K - 1, CONV_WIDTH)),
        'state_pool': nrm(ks[8], (DEPTH, DEC_BATCH, POOL_HIST, POOL_WIDTH)),
        'page_table': page_table,
        'g_pre_mix': gain(ks[10], D_MODEL),
        'w_in': nrm(ks[11], (DEPTH, D_MODEL, PROJ_WIDTH), D_MODEL ** -0.5) * col_scale,
        'b_f': FORGET_BIAS_INIT + nrm(ks[12], (DEPTH, ATT_HEADS), 0.5),
        'conv_w': nrm(ks[13], (DEPTH, CONV_K, CONV_WIDTH), CONV_K ** -0.5),
        'pool_w': nrm(ks[14], (DEPTH, POOL_GROUPS, POOL_GROUP_DIM, POOL_GROUP_DIM), POOL_GROUP_DIM ** -0.5),
        'pool_scale': 1.0 + nrm(ks[15], (DEPTH, POOL_WIDTH), 0.1),
        'g_att': gain(ks[16], ATT_WIDTH),
        'g_conv': gain(ks[17], CONV_WIDTH),
        'g_pool': gain(ks[18], POOL_WIDTH),
        'w_o': nrm(ks[19], (DEPTH, MIX_WIDTH, D_MODEL), MIX_WIDTH ** -0.5),
        'g_post_mix': gain(ks[20], D_MODEL),
        'g_pre_ffn': gain(ks[21], D_MODEL),
        'w_gu': nrm(ks[22], (DEPTH, D_MODEL, 2 * D_FF), D_MODEL ** -0.5),
        'w_down': nrm(ks[23], (DEPTH, D_FF, D_MODEL), D_FF ** -0.5),
        'g_post_ffn': gain(ks[24], D_MODEL),
        'w_ple_proj': nrm(ks[25], (DEPTH, PLE_DIM, D_MODEL), PLE_DIM ** -0.5),
        'w_ple_gate': nrm(ks[26], (DEPTH, D_MODEL, D_MODEL), D_MODEL ** -0.5),
    }


def reference(x_prompt, x_sample, p_prompt, p_sample, cache_k, cache_v, cache_logf, state_conv, state_pool,
              page_table, g_pre_mix, w_in, b_f, conv_w, pool_w, pool_scale, g_att, g_conv, g_pool, w_o,
              g_post_mix, g_pre_ffn, w_gu, w_down, g_post_ffn, w_ple_proj, w_ple_gate):
    b_p = x_prompt.shape[0]
    b_s = x_sample.shape[0]
    n_past = page_table.shape[1] * PAGE_SIZE
    conv0 = jnp.zeros((b_p, CONV_K - 1, CONV_WIDTH), x_prompt.dtype)
    pool0 = jnp.zeros((b_p, POOL_HIST, POOL_WIDTH), x_prompt.dtype)
    yp, ys = x_prompt, x_sample
    st_prompt, st_sample = [], []
    for l in range(DEPTH):
        wl = dict(g_pre_mix=g_pre_mix[l], w_in=w_in[l], b_f=b_f[l], conv_w=conv_w[l], pool_w=pool_w[l],
                  pool_scale=pool_scale[l], g_att=g_att[l], g_conv=g_conv[l], g_pool=g_pool[l], w_o=w_o[l],
                  g_post_mix=g_post_mix[l], g_pre_ffn=g_pre_ffn[l], w_gu=w_gu[l], w_down=w_down[l],
                  g_post_ffn=g_post_ffn[l], w_ple_proj=w_ple_proj[l], w_ple_gate=w_ple_gate[l])
        yp, sp = trunk_layer(yp, p_prompt[l], wl, fox_prompt, conv0, pool0, 0)
        k_past = cache_k[l][page_table].reshape(b_s, n_past, ATT_HEADS, HEAD_DIM)
        v_past = cache_v[l][page_table].reshape(b_s, n_past, ATT_HEADS, HEAD_DIM)
        lf_past = cache_logf[l][page_table].reshape(b_s, n_past, ATT_HEADS)
        attend = functools.partial(fox_sample, k_past=k_past, v_past=v_past, logf_past=lf_past)
        ys, ss = trunk_layer(ys, p_sample[l], wl, attend, state_conv[l], state_pool[l], n_past)
        st_prompt.append(sp)
        st_sample.append(ss)
    k_prompt = jnp.stack([s[0] for s in st_prompt])
    v_prompt = jnp.stack([s[1] for s in st_prompt])
    logf_prompt = jnp.stack([s[2] for s in st_prompt])
    conv_prompt = jnp.stack([s[3] for s in st_prompt])
    pool_prompt = jnp.stack([s[4] for s in st_prompt])
    k_sample = jnp.stack([s[0] for s in st_sample])
    v_sample = jnp.stack([s[1] for s in st_sample])
    logf_sample = jnp.stack([s[2] for s in st_sample])
    conv_sample = jnp.stack([s[3] for s in st_sample])
    pool_sample = jnp.stack([s[4] for s in st_sample])
    return (yp, ys, k_prompt, v_prompt, logf_prompt, conv_prompt, pool_prompt,
            k_sample, v_sample, logf_sample, conv_sample, pool_sample)
```

```python
import functools

import jax
import jax.numpy as jnp
from jax import lax
from jax.experimental import pallas as pl
from jax.experimental.pallas import tpu as pltpu

D_MODEL = 1024
ATT_HEADS = 8
HEAD_DIM = 64
ATT_WIDTH = ATT_HEADS * HEAD_DIM
CONV_WIDTH = 256
CONV_K = 3
POOL_WINDOWS = (2, 4, 8, 16)
POOL_WIDTH = 256
POOL_GROUP_DIM = POOL_WIDTH // len(POOL_WINDOWS)
POOL_HIST = max(POOL_WINDOWS) - 1
D_FF = 2816
PLE_DIM = 256
PAGE_SIZE = 128
RMS_EPS = 1e-6
Q_SCALE = HEAD_DIM ** -0.5

LANES = 128
SUBLANES = 8
VMEM_LIMIT_BYTES = 56 * 1024 * 1024

RQ, RK, RV = 0, ATT_WIDTH, 2 * ATT_WIDTH
RCB = 3 * ATT_WIDTH
RCC = RCB + CONV_WIDTH
RCH = RCC + CONV_WIDTH
RPU = RCH + CONV_WIDTH
RF = RPU + POOL_WIDTH
R_WIDTH = RF + LANES
F_REP = 6
F_LANES = F_REP * ATT_HEADS

NEG = float(jnp.finfo(jnp.float32).min)
BF16 = jnp.bfloat16
F32 = jnp.float32


def _rms(x, g):
    return x * lax.rsqrt(jnp.mean(x * x, axis=-1, keepdims=True) + RMS_EPS) * g


def _log_sigmoid(t):
    return jnp.minimum(t, 0.0) - jnp.log1p(jnp.exp(-jnp.abs(t)))


def _split3(c):
    hi = c.astype(BF16).astype(F32)
    r1 = c - hi
    mid = r1.astype(BF16).astype(F32)
    lo = (r1 - mid).astype(BF16).astype(F32)
    return hi, mid, lo


def _dot(a, b):
    return jnp.dot(a, b, preferred_element_type=F32)


def _dot_nt(a, b):
    return lax.dot_general(a, b, (((1,), (1,)), ((), ())), preferred_element_type=F32)


def _pool_window_select(s2, s4, s8, s16, lane):
    return jnp.where(lane < POOL_GROUP_DIM, s2,
                     jnp.where(lane < 2 * POOL_GROUP_DIM, s4,
                               jnp.where(lane < 3 * POOL_GROUP_DIM, s8, s16)))


def _pool_window_width(lane):
    return jnp.where(lane < POOL_GROUP_DIM, POOL_WINDOWS[0],
                     jnp.where(lane < 2 * POOL_GROUP_DIM, POOL_WINDOWS[1],
                               jnp.where(lane < 3 * POOL_GROUP_DIM, POOL_WINDOWS[2], POOL_WINDOWS[3])))


def _inproj_prompt_kernel(x_ref, gpre_ref, w_ref, bf_ref, convw_ref, poolw_ref, pscale_ref, gconv_ref,
                          gpool_ref,
                          q_ref, k16_ref, v16_ref, aq_ref, ak_ref, k32_ref, v32_ref, logf_ref, cp_ref,
                          convn_ref, pooln_ref,
                          carry_ref, gbuf_ref, ubuf_ref, *, ts):
    i = pl.program_id(1)

    @pl.when(i == 0)
    def _():
        carry_ref[...] = jnp.zeros_like(carry_ref)
        gbuf_ref[0:SUBLANES, :] = jnp.zeros((SUBLANES, CONV_WIDTH), F32)
        ubuf_ref[0:2 * SUBLANES, :] = jnp.zeros((2 * SUBLANES, POOL_WIDTH), F32)

    x = x_ref[...]
    h = _rms(x, gpre_ref[...]).astype(BF16)
    z = _dot(h, w_ref[...])

    zk = z[:, RK:RV]
    zv = z[:, RV:RCB]
    q_ref[...] = (z[:, RQ:RK] * Q_SCALE).astype(BF16)
    k16_ref[...] = zk.astype(BF16)
    v16_ref[...] = zv.astype(BF16)
    k32_ref[...] = zk
    v32_ref[...] = zv

    lf = _log_sigmoid(z[:, RF:R_WIDTH] + bf_ref[...])
    logf_ref[...] = lf[:, 0:ATT_HEADS]
    row = lax.broadcasted_iota(jnp.int32, (ts, LANES), 0)
    lane = lax.broadcasted_iota(jnp.int32, (ts, LANES), 1)
    c = lf
    s = 1
    while s < ts:
        c = c + jnp.where(row >= s, pltpu.roll(c, s, 0), 0.0)
        s *= 2
    c = c + carry_ref[...]
    carry_ref[...] = c[ts - 1:ts, :]
    hi, mid, lo = _split3(c)
    aq = jnp.where(lane < 8, hi, jnp.where(lane < 16, mid, jnp.where(lane < 24, lo,
                                                                     jnp.where(lane < F_LANES, 1.0, 0.0))))
    ak = jnp.where(lane < 24, 1.0, jnp.where(lane < 32, -hi, jnp.where(lane < 40, -mid,
                                                                       jnp.where(lane < F_LANES, -lo, 0.0))))
    aq_ref[...] = aq.astype(BF16)
    ak_ref[...] = ak.astype(BF16)

    g = z[:, RCC:RCH] * z[:, RCH:RPU]
    gbuf_ref[SUBLANES:SUBLANES + ts, :] = g
    cw = convw_ref[...]
    conv = (cw[0:1, :] * gbuf_ref[SUBLANES - 2:SUBLANES - 2 + ts, :]
            + cw[1:2, :] * gbuf_ref[SUBLANES - 1:SUBLANES - 1 + ts, :]
            + cw[2:3, :] * g)
    conv = z[:, RCB:RCC] * conv
    convn_ref[...] = gbuf_ref[SUBLANES + ts - (CONV_K - 1):SUBLANES + ts, :]
    gbuf_ref[0:SUBLANES, :] = gbuf_ref[ts:ts + SUBLANES, :]

    u = z[:, RPU:RF]
    hist = 2 * SUBLANES
    ubuf_ref[hist:hist + ts, :] = u
    f0 = ubuf_ref[...]
    s2 = f0 + pltpu.roll(f0, 1, 0)
    s4 = s2 + pltpu.roll(s2, 2, 0)
    s8 = s4 + pltpu.roll(s4, 4, 0)
    s16 = s8 + pltpu.roll(s8, 8, 0)
    lane_p = lax.broadcasted_iota(jnp.int32, (ts, POOL_WIDTH), 1)
    row_p = lax.broadcasted_iota(jnp.int32, (ts, POOL_WIDTH), 0)
    win = _pool_window_select(s2[hist:], s4[hist:], s8[hist:], s16[hist:], lane_p)
    cnt = jnp.minimum(i * ts + row_p + 1, _pool_window_width(lane_p)).astype(F32)
    diff = win / cnt - u
    pool = _dot(diff.astype(BF16), poolw_ref[...]) * pscale_ref[...]
    pooln_ref[...] = ubuf_ref[hist + ts - POOL_HIST:hist + ts, :]
    ubuf_ref[0:hist, :] = ubuf_ref[ts:ts + hist, :]

    cp_ref[:, 0:CONV_WIDTH] = _rms(conv, gconv_ref[...]).astype(BF16)
    cp_ref[:, CONV_WIDTH:CONV_WIDTH + POOL_WIDTH] = _rms(pool, gpool_ref[...]).astype(BF16)


def _inproj_prompt(x, gpre, w_r, bf_r, convw, poolw, pscale, gconv, gpool, *, ts):
    b, s, d = x.shape
    n = s // ts
    row_spec = lambda w: pl.BlockSpec((None, ts, w), lambda bi, i: (bi, i, 0))
    full = lambda a: pl.BlockSpec(a.shape, lambda bi, i: (0,) * a.ndim)
    out_shape = (
        jax.ShapeDtypeStruct((b, s, ATT_WIDTH), BF16),
        jax.ShapeDtypeStruct((b, s, ATT_WIDTH), BF16),
        jax.ShapeDtypeStruct((b, s, ATT_WIDTH), BF16),
        jax.ShapeDtypeStruct((b, s, LANES), BF16),
        jax.ShapeDtypeStruct((b, s, LANES), BF16),
        jax.ShapeDtypeStruct((b, s, ATT_WIDTH), F32),
        jax.ShapeDtypeStruct((b, s, ATT_WIDTH), F32),
        jax.ShapeDtypeStruct((b, s, ATT_HEADS), F32),
        jax.ShapeDtypeStruct((b, s, CONV_WIDTH + POOL_WIDTH), BF16),
        jax.ShapeDtypeStruct((b, CONV_K - 1, CONV_WIDTH), F32),
        jax.ShapeDtypeStruct((b, POOL_HIST, POOL_WIDTH), F32),
    )
    out_specs = (
        row_spec(ATT_WIDTH), row_spec(ATT_WIDTH), row_spec(ATT_WIDTH), row_spec(LANES), row_spec(LANES),
        row_spec(ATT_WIDTH), row_spec(ATT_WIDTH), row_spec(ATT_HEADS), row_spec(CONV_WIDTH + POOL_WIDTH),
        pl.BlockSpec((None, CONV_K - 1, CONV_WIDTH), lambda bi, i: (bi, 0, 0)),
        pl.BlockSpec((None, POOL_HIST, POOL_WIDTH), lambda bi, i: (bi, 0, 0)),
    )
    return pl.pallas_call(
        functools.partial(_inproj_prompt_kernel, ts=ts),
        grid=(b, n),
        in_specs=[row_spec(d), full(gpre), full(w_r), full(bf_r), full(convw), full(poolw), full(pscale),
                  full(gconv), full(gpool)],
        out_specs=out_specs,
        out_shape=out_shape,
        scratch_shapes=[pltpu.VMEM((1, LANES), F32),
                        pltpu.VMEM((ts + SUBLANES, CONV_WIDTH), F32),
                        pltpu.VMEM((ts + 2 * SUBLANES, POOL_WIDTH), F32)],
        compiler_params=pltpu.CompilerParams(dimension_semantics=("parallel", "arbitrary"),
                                             vmem_limit_bytes=VMEM_LIMIT_BYTES),
        name="inproj_prompt",
    )(x, gpre, w_r, bf_r, convw, poolw, pscale, gconv, gpool)


def _attn_prompt_kernel(q_ref, aq_ref, k_ref, ak_ref, v_ref, o_ref, *, blk):
    pair = pl.program_id(1)
    s_len = q_ref.shape[0]
    n_blk = s_len // blk
    lane = lax.broadcasted_iota(jnp.int32, (blk, LANES), 1)
    r_id = lax.broadcasted_iota(jnp.int32, (blk, blk), 0)
    c_id = lax.broadcasted_iota(jnp.int32, (blk, blk), 1)
    causal = r_id >= c_id

    def q_block(qi, _):
        q0 = pl.multiple_of(qi * blk, blk)
        qd = q_ref[pl.ds(q0, blk), :]
        qa = aq_ref[pl.ds(q0, blk), :]
        outs = []
        for e in range(2):
            in_half = (lane >= e * HEAD_DIM) & (lane < (e + 1) * HEAD_DIM)
            head = 2 * pair + e
            lhs = jnp.concatenate(
                [jnp.where(in_half, qd, jnp.zeros_like(qd)),
                 jnp.where(((lane & (ATT_HEADS - 1)) == head) & (lane < F_LANES), qa, jnp.zeros_like(qa))],
                axis=1)

            def scores(kv):
                k0 = pl.multiple_of(kv * blk, blk)
                rhs = jnp.concatenate([k_ref[pl.ds(k0, blk), :], ak_ref[pl.ds(k0, blk), :]], axis=1)
                return _dot_nt(lhs, rhs), v_ref[pl.ds(k0, blk), :]

            def update(carry, sc, vb):
                m, l, acc = carry
                m_new = jnp.maximum(m, jnp.max(sc, axis=-1, keepdims=True))
                alpha = jnp.exp(m - m_new)
                p = jnp.exp(sc - m_new)
                l = alpha * l + jnp.sum(p, axis=-1, keepdims=True)
                acc = alpha * acc + _dot(p.astype(BF16), vb)
                return m_new, l, acc

            def kv_block(kv, carry):
                sc, vb = scores(kv)
                return update(carry, sc, vb)

            init = (jnp.full((blk, 1), NEG, F32), jnp.zeros((blk, 1), F32), jnp.zeros((blk, LANES), F32))
            carry = lax.fori_loop(0, qi, kv_block, init)
            sc, vb = scores(qi)
            m, l, acc = update(carry, jnp.where(causal, sc, NEG), vb)
            outs.append(acc / l)
        o_ref[pl.ds(q0, blk), :] = jnp.where(lane < HEAD_DIM, outs[0], outs[1])
        return 0

    lax.fori_loop(0, n_blk, q_block, 0)


def _attn_prompt(q, aq, k16, ak, v16, *, blk):
    b, s, _ = q.shape
    pairs = ATT_WIDTH // LANES
    pair_spec = pl.BlockSpec((None, s, LANES), lambda bi, j: (bi, 0, j))
    aug_spec = pl.BlockSpec((None, s, LANES), lambda bi, j: (bi, 0, 0))
    return pl.pallas_call(
        functools.partial(_attn_prompt_kernel, blk=blk),
        grid=(b, pairs),
        in_specs=[pair_spec, aug_spec, pair_spec, aug_spec, pair_spec],
        out_specs=pair_spec,
        out_shape=jax.ShapeDtypeStruct((b, s, ATT_WIDTH), F32),
        compiler_params=pltpu.CompilerParams(dimension_semantics=("parallel", "parallel"),
                                             vmem_limit_bytes=VMEM_LIMIT_BYTES),
        name="attn_prompt",
    )(q, aq, k16, ak, v16)


FF_CHUNK = 1408


def _outffn_kernel(x_ref, att_ref, cp_ref, ple_ref, gatt_ref, wo_ref, gpm_ref, gpf_ref, wgu_ref, wd_ref,
                   gpo_ref, wpp_ref, wpg_ref, y_ref):
    att_n = _rms(att_ref[...], gatt_ref[...]).astype(BF16)
    o = _dot(att_n, wo_ref[0:ATT_WIDTH, :]) + _dot(cp_ref[...], wo_ref[ATT_WIDTH:, :])
    x1 = x_ref[...] + _rms(o, gpm_ref[...])
    h2 = _rms(x1, gpf_ref[...]).astype(BF16)
    f = None
    for c0 in range(0, D_FF, FF_CHUNK):
        gate = _dot(h2, wgu_ref[:, c0:c0 + FF_CHUNK])
        up = _dot(h2, wgu_ref[:, D_FF + c0:D_FF + c0 + FF_CHUNK])
        part = _dot((gate * jax.nn.sigmoid(gate) * up).astype(BF16), wd_ref[c0:c0 + FF_CHUNK, :])
        f = part if f is None else f + part
    x2 = x1 + _rms(f, gpo_ref[...])
    emb = _dot(ple_ref[...].astype(BF16), wpp_ref[...])
    y_ref[...] = x2 + emb * jax.nn.sigmoid(_dot(x2.astype(BF16), wpg_ref[...]))


def _outffn(x, att, cp, ple, gatt, wo, gpm, gpf, wgu, wd, gpo, wpp, wpg, *, tm):
    t, d = x.shape
    row_spec = lambda w: pl.BlockSpec((tm, w), lambda i: (i, 0))
    full = lambda a: pl.BlockSpec(a.shape, lambda i: (0,) * a.ndim, pipeline_mode=pl.Buffered(1))
    return pl.pallas_call(
        _outffn_kernel,
        grid=(t // tm,),
        in_specs=[row_spec(d), row_spec(ATT_WIDTH), row_spec(CONV_WIDTH + POOL_WIDTH), row_spec(PLE_DIM),
                  full(gatt), full(wo), full(gpm), full(gpf), full(wgu), full(wd), full(gpo), full(wpp),
                  full(wpg)],
        out_specs=row_spec(d),
        out_shape=jax.ShapeDtypeStruct((t, d), F32),
        compiler_params=pltpu.CompilerParams(dimension_semantics=("parallel",),
                                             vmem_limit_bytes=VMEM_LIMIT_BYTES),
        name="outffn",
    )(x, att, cp, ple, gatt, wo, gpm, gpf, wgu, wd, gpo, wpp, wpg)


def _inproj_sample_kernel(xbm_ref, xtm_ref, gpre_ref, w_ref, wt_ref, bf_ref, bfcol_ref, convw_ref, poolw_ref,
                          pscale_ref, gconv_ref, gpool_ref, chist_ref, phist_ref,
                          q_ref, k32_ref, v32_ref, logf_ref, kt_ref, vt_ref, cumt_ref, cp_ref, convn_ref,
                          pooln_ref, *, n_b, n_t, pos0):
    rows = n_b * n_t
    h_bm = _rms(xbm_ref[...], gpre_ref[...]).astype(BF16)
    h_tm = _rms(xtm_ref[...], gpre_ref[...]).astype(BF16)

    z = _dot(h_bm, w_ref[:, RQ:RCB])
    q_ref[...] = z[:, RQ:RK] * Q_SCALE
    k32_ref[...] = z[:, RK:RV]
    v32_ref[...] = z[:, RV:RCB]
    zf = _dot(h_bm, w_ref[:, RF:R_WIDTH])
    logf_ref[...] = _log_sigmoid(zf + bf_ref[...])[:, 0:ATT_HEADS]

    zt = _dot_nt(wt_ref[...], h_bm)
    kt_ref[...] = zt[0:ATT_WIDTH, :].astype(BF16)
    vt_ref[...] = zt[ATT_WIDTH:2 * ATT_WIDTH, :].astype(BF16)
    lft = _log_sigmoid(zt[2 * ATT_WIDTH:, :] + bfcol_ref[...])
    lane = lax.broadcasted_iota(jnp.int32, lft.shape, 1)
    t_of = lane % n_t
    c = lft
    s = 1
    while s < n_t:
        c = c + jnp.where(t_of >= s, pltpu.roll(c, s, 1), 0.0)
        s *= 2
    cumt_ref[...] = c

    zc = _dot(h_tm, w_ref[:, RCB:RF])
    g = zc[:, CONV_WIDTH:2 * CONV_WIDTH] * zc[:, 2 * CONV_WIDTH:3 * CONV_WIDTH]
    full_g = jnp.concatenate([chist_ref[...], g], axis=0)
    cw = convw_ref[...]
    conv = sum(cw[j:j + 1, :] * full_g[j * n_b:j * n_b + rows, :] for j in range(CONV_K))
    conv = zc[:, 0:CONV_WIDTH] * conv
    convn_ref[...] = full_g[rows:, :]

    u = zc[:, 3 * CONV_WIDTH:]
    full_u = jnp.concatenate([phist_ref[...], u], axis=0)
    pooln_ref[...] = full_u[rows:, :]
    sums = []
    cur, off = full_u, 0
    for w in POOL_WINDOWS:
        sh = (w // 2) * n_b
        cur = cur[sh:, :] + cur[:cur.shape[0] - sh, :]
        off += sh
        sums.append(cur[POOL_HIST * n_b - off:POOL_HIST * n_b - off + rows, :])
    lane_p = lax.broadcasted_iota(jnp.int32, (rows, POOL_WIDTH), 1)
    row_p = lax.broadcasted_iota(jnp.int32, (rows, POOL_WIDTH), 0)
    win = _pool_window_select(*sums, lane_p)
    cnt = jnp.minimum(pos0 + row_p // n_b + 1, _pool_window_width(lane_p)).astype(F32)
    diff = win / cnt - u
    pool = _dot(diff.astype(BF16), poolw_ref[...]) * pscale_ref[...]

    cp_ref[:, 0:CONV_WIDTH] = _rms(conv, gconv_ref[...]).astype(BF16)
    cp_ref[:, CONV_WIDTH:CONV_WIDTH + POOL_WIDTH] = _rms(pool, gpool_ref[...]).astype(BF16)


def _inproj_sample(xbm, xtm, gpre, w_r, wt, bf_r, bfcol, convw, poolw, pscale, gconv, gpool, chist, phist,
                   *, n_b, n_t, pos0):
    rows = n_b * n_t
    out_shape = (
        jax.ShapeDtypeStruct((rows, ATT_WIDTH), F32),
        jax.ShapeDtypeStruct((rows, ATT_WIDTH), F32),
        jax.ShapeDtypeStruct((rows, ATT_WIDTH), F32),
        jax.ShapeDtypeStruct((rows, ATT_HEADS), F32),
        jax.ShapeDtypeStruct((ATT_WIDTH, rows), BF16),
        jax.ShapeDtypeStruct((ATT_WIDTH, rows), BF16),
        jax.ShapeDtypeStruct((ATT_HEADS, rows), F32),
        jax.ShapeDtypeStruct((rows, CONV_WIDTH + POOL_WIDTH), BF16),
        jax.ShapeDtypeStruct(((CONV_K - 1) * n_b, CONV_WIDTH), F32),
        jax.ShapeDtypeStruct((POOL_HIST * n_b, POOL_WIDTH), F32),
    )
    return pl.pallas_call(
        functools.partial(_inproj_sample_kernel, n_b=n_b, n_t=n_t, pos0=pos0),
        out_shape=out_shape,
        compiler_params=pltpu.CompilerParams(vmem_limit_bytes=VMEM_LIMIT_BYTES),
        name="inproj_sample",
    )(xbm, xtm, gpre, w_r, wt, bf_r, bfcol, convw, poolw, pscale, gconv, gpool, chist, phist)


PAGES_PER_GROUP = 4


def _attn_sample_kernel(pt_ref, q_ref, cumt_ref, ktn_ref, vtn_ref, kc_ref, vc_ref, lfc_ref, o_ref,
                        kbuf, vbuf, lfbuf, tail_ref, ksem, vsem, lfsem, *, layer, n_t, n_pages):
    b = pl.program_id(0)
    grp = PAGES_PER_GROUP
    n_groups = n_pages // grp
    rows = n_t * ATT_HEADS

    def lf_copy(p):
        return pltpu.make_async_copy(lfc_ref.at[layer, pt_ref[b, p]], lfbuf.at[p], lfsem)

    def kv_copies(g, slot, i):
        page = pt_ref[b, g * grp + i]
        return (pltpu.make_async_copy(kc_ref.at[layer, page], kbuf.at[slot, i], ksem.at[slot, i]),
                pltpu.make_async_copy(vc_ref.at[layer, page], vbuf.at[slot, i], vsem.at[slot, i]))

    def start_group(g, slot):
        for i in range(grp):
            for cp in kv_copies(g, slot, i):
                cp.start()

    def wait_group(g, slot):
        for i in range(grp):
            for cp in kv_copies(g, slot, i):
                cp.wait()

    def lf_start(p, _):
        lf_copy(p).start()
        return 0

    def lf_wait(p, _):
        lf_copy(p).wait()
        return 0

    lax.fori_loop(0, n_pages, lf_start, 0)
    start_group(0, 0)
    lax.fori_loop(0, n_pages, lf_wait, 0)

    lp = lfbuf[...].reshape(n_pages * ATT_HEADS, PAGE_SIZE)
    after = (lax.broadcasted_iota(jnp.int32, (PAGE_SIZE, PAGE_SIZE), 0)
             > lax.broadcasted_iota(jnp.int32, (PAGE_SIZE, PAGE_SIZE), 1)).astype(BF16)
    within = sum(_dot(piece.astype(BF16), after) for piece in _split3(lp))
    tail_ref[...] = within.reshape(n_pages, ATT_HEADS, PAGE_SIZE)

    def tail_step(k, later):
        p = n_pages - 1 - k
        w = tail_ref[p]
        tail_ref[p] = w + later
        return later + (w[:, 0:1] + lfbuf[p][:, 0:1])

    lax.fori_loop(0, n_pages, tail_step, jnp.zeros((ATT_HEADS, 1), F32))

    q_b = q_ref[...]
    sub = lax.broadcasted_iota(jnp.int32, (ATT_HEADS, ATT_WIDTH), 0)
    col_head = lax.broadcasted_iota(jnp.int32, (ATT_HEADS, ATT_WIDTH), 1) // HEAD_DIM
    diag = sub == col_head
    qbd = jnp.concatenate(
        [jnp.where(diag, jnp.broadcast_to(q_b[t:t + 1, :], (ATT_HEADS, ATT_WIDTH)), 0.0) for t in range(n_t)],
        axis=0).astype(BF16)
    cumt = cumt_ref[...]
    lane = lax.broadcasted_iota(jnp.int32, cumt.shape, 1)
    cq = jnp.concatenate(
        [jnp.sum(jnp.where(lane == b * n_t + t, cumt, 0.0), axis=1, keepdims=True) for t in range(n_t)],
        axis=0)

    def update(carry, sc, vt):
        m, l, acc = carry
        m_new = jnp.maximum(m, jnp.max(sc, axis=-1, keepdims=True))
        alpha = jnp.exp(m - m_new)
        p = jnp.exp(sc - m_new)
        l = alpha * l + jnp.sum(p, axis=-1, keepdims=True)
        acc = alpha * acc + _dot_nt(p.astype(BF16), vt)
        return m_new, l, acc

    def group_step(g, carry):
        slot = g % 2

        @pl.when(g + 1 < n_groups)
        def _():
            start_group(g + 1, 1 - slot)

        wait_group(g, slot)
        for i in range(grp):
            kt = kbuf[slot, i].astype(BF16)
            vt = vbuf[slot, i].astype(BF16)
            tail = tail_ref[g * grp + i]
            sc = _dot(qbd, kt) + cq + jnp.concatenate([tail] * n_t, axis=0)
            carry = update(carry, sc, vt)
        return carry

    init = (jnp.full((rows, 1), NEG, F32), jnp.zeros((rows, 1), F32), jnp.zeros((rows, ATT_WIDTH), F32))
    carry = lax.fori_loop(0, n_groups, group_step, init)

    sc = _dot(qbd, ktn_ref[...]) + cq - jnp.concatenate([cumt] * n_t, axis=0)
    lane_r = lax.broadcasted_iota(jnp.int32, sc.shape, 1)
    t_row = lax.broadcasted_iota(jnp.int32, sc.shape, 0) // ATT_HEADS
    valid = (lane_r >= b * n_t) & (lane_r <= b * n_t + t_row)
    m, l, acc = update(carry, jnp.where(valid, sc, NEG), vtn_ref[...])

    out = acc / l
    o_ref[...] = jnp.concatenate(
        [jnp.sum(jnp.where(diag, out[t * ATT_HEADS:(t + 1) * ATT_HEADS, :], 0.0), axis=0, keepdims=True)
         for t in range(n_t)], axis=0)


def _attn_sample(page_table, q, cumt, ktn, vtn, kc, vc, lfc, *, layer):
    n_b, n_pages = page_table.shape
    n_t = q.shape[1]
    grp = PAGES_PER_GROUP
    grid_spec = pltpu.PrefetchScalarGridSpec(
        num_scalar_prefetch=1,
        grid=(n_b,),
        in_specs=[
            pl.BlockSpec((None, n_t, ATT_WIDTH), lambda b, pt: (b, 0, 0)),
            pl.BlockSpec(cumt.shape, lambda b, pt: (0, 0)),
            pl.BlockSpec(ktn.shape, lambda b, pt: (0, 0)),
            pl.BlockSpec(vtn.shape, lambda b, pt: (0, 0)),
            pl.BlockSpec(memory_space=pl.ANY),
            pl.BlockSpec(memory_space=pl.ANY),
            pl.BlockSpec(memory_space=pl.ANY),
        ],
        out_specs=pl.BlockSpec((None, n_t, ATT_WIDTH), lambda b, pt: (b, 0, 0)),
        scratch_shapes=[
            pltpu.VMEM((2, grp, ATT_WIDTH, PAGE_SIZE), F32),
            pltpu.VMEM((2, grp, ATT_WIDTH, PAGE_SIZE), F32),
            pltpu.VMEM((n_pages, ATT_HEADS, PAGE_SIZE), F32),
            pltpu.VMEM((n_pages, ATT_HEADS, PAGE_SIZE), F32),
            pltpu.SemaphoreType.DMA((2, grp)),
            pltpu.SemaphoreType.DMA((2, grp)),
            pltpu.SemaphoreType.DMA(()),
        ],
    )
    return pl.pallas_call(
        functools.partial(_attn_sample_kernel, layer=layer, n_t=n_t, n_pages=n_pages),
        grid_spec=grid_spec,
        out_shape=jax.ShapeDtypeStruct((n_b, n_t, ATT_WIDTH), F32),
        compiler_params=pltpu.CompilerParams(dimension_semantics=("arbitrary",),
                                             vmem_limit_bytes=VMEM_LIMIT_BYTES),
        name="attn_sample",
    )(page_table, q, cumt, ktn, vtn, kc, vc, lfc)


def _prep_layer(l, w_in, b_f, pool_w, w_o, w_gu, w_down, w_ple_proj, w_ple_gate):
    wi = w_in[l]
    q_off, k_off, v_off = 0, ATT_WIDTH, 2 * ATT_WIDTH
    f_off = 3 * ATT_WIDTH
    rest = f_off + ATT_HEADS
    wf = wi[:, f_off:rest]
    f_slab = jnp.concatenate([wf] * F_REP + [jnp.zeros((D_MODEL, LANES - F_LANES), F32)], axis=1)
    w_r = jnp.concatenate([wi[:, q_off:f_off], wi[:, rest:], f_slab], axis=1).astype(BF16)
    wt = jnp.concatenate([wi[:, k_off:f_off], wf], axis=1).T.astype(BF16)
    bf = b_f[l]
    bf_r = jnp.concatenate([bf] * F_REP + [jnp.zeros((LANES - F_LANES,), F32)])[None, :]
    bfcol = bf[:, None]
    pw = pool_w[l]
    n_g = len(POOL_WINDOWS)
    poolw = jnp.zeros((n_g, POOL_GROUP_DIM, n_g, POOL_GROUP_DIM), F32)
    poolw = poolw.at[jnp.arange(n_g), :, jnp.arange(n_g), :].set(pw).reshape(POOL_WIDTH, POOL_WIDTH)
    return dict(w_r=w_r, wt=wt, bf_r=bf_r, bfcol=bfcol, poolw=poolw.astype(BF16),
                wo=w_o[l].astype(BF16), wgu=w_gu[l].astype(BF16), wd=w_down[l].astype(BF16),
                wpp=w_ple_proj[l].astype(BF16), wpg=w_ple_gate[l].astype(BF16))


def kernel(x_prompt, x_sample, p_prompt, p_sample, cache_k, cache_v, cache_logf, state_conv, state_pool,
           page_table, g_pre_mix, w_in, b_f, conv_w, pool_w, pool_scale, g_att, g_conv, g_pool, w_o,
           g_post_mix, g_pre_ffn, w_gu, w_down, g_post_ffn, w_ple_proj, w_ple_gate):
    depth = w_in.shape[0]
    b_p, s_p, _ = x_prompt.shape
    b_s, t_s, _ = x_sample.shape
    n_pool = cache_k.shape[1]
    n_past = page_table.shape[1] * PAGE_SIZE
    rows_s = b_s * t_s

    kc = jnp.transpose(cache_k, (0, 1, 3, 4, 2)).reshape(depth, n_pool, ATT_WIDTH, PAGE_SIZE)
    vc = jnp.transpose(cache_v, (0, 1, 3, 4, 2)).reshape(depth, n_pool, ATT_WIDTH, PAGE_SIZE)
    lfc = jnp.transpose(cache_logf, (0, 1, 3, 2))

    def to_tm(a):
        return jnp.swapaxes(a, 0, 1).reshape(a.shape[0] * a.shape[1], a.shape[2])

    def to_bm(a, n_t):
        return jnp.swapaxes(a.reshape(n_t, b_s, a.shape[-1]), 0, 1)

    yp = x_prompt
    ys = x_sample.reshape(rows_s, D_MODEL)
    outs = {k: [] for k in ("kp", "vp", "lfp", "cvp", "plp", "ks", "vs", "lfs", "cvs", "pls")}
    row2 = lambda a: a[None, :]
    for l in range(depth):
        w = _prep_layer(l, w_in, b_f, pool_w, w_o, w_gu, w_down, w_ple_proj, w_ple_gate)
        gains = (row2(g_att[l]), w["wo"], row2(g_post_mix[l]), row2(g_pre_ffn[l]), w["wgu"], w["wd"],
                 row2(g_post_ffn[l]), w["wpp"], w["wpg"])
        mix_w = (conv_w[l], w["poolw"], row2(pool_scale[l]), row2(g_conv[l]), row2(g_pool[l]))

        q, k16, v16, aq, ak, k32, v32, lf, cp, cvn, pln = _inproj_prompt(
            yp, row2(g_pre_mix[l]), w["w_r"], w["bf_r"], *mix_w, ts=512)
        att = _attn_prompt(q, aq, k16, ak, v16, blk=256)
        yp = _outffn(yp.reshape(b_p * s_p, D_MODEL), att.reshape(b_p * s_p, ATT_WIDTH),
                     cp.reshape(b_p * s_p, -1), p_prompt[l].reshape(b_p * s_p, PLE_DIM), *gains,
                     tm=512).reshape(b_p, s_p, D_MODEL)
        outs["kp"].append(k32.reshape(b_p, s_p, ATT_HEADS, HEAD_DIM))
        outs["vp"].append(v32.reshape(b_p, s_p, ATT_HEADS, HEAD_DIM))
        outs["lfp"].append(lf)
        outs["cvp"].append(cvn)
        outs["plp"].append(pln)

        xtm = to_tm(ys.reshape(b_s, t_s, D_MODEL))
        qs, ks32, vs32, lfs, ktn, vtn, cumt, cps, cvs, pls = _inproj_sample(
            ys, xtm, row2(g_pre_mix[l]), w["w_r"], w["wt"], w["bf_r"], w["bfcol"], *mix_w,
            to_tm(state_conv[l]), to_tm(state_pool[l]), n_b=b_s, n_t=t_s, pos0=n_past)
        att_s = _attn_sample(page_table, qs.reshape(b_s, t_s, ATT_WIDTH), cumt, ktn, vtn, kc, vc, lfc,
                             layer=l)
        ys = _outffn(ys, att_s.reshape(rows_s, ATT_WIDTH), to_bm(cps, t_s).reshape(rows_s, -1),
                     p_sample[l].reshape(rows_s, PLE_DIM), *gains, tm=rows_s)
        outs["ks"].append(ks32.reshape(b_s, t_s, ATT_HEADS, HEAD_DIM))
        outs["vs"].append(vs32.reshape(b_s, t_s, ATT_HEADS, HEAD_DIM))
        outs["lfs"].append(lfs.reshape(b_s, t_s, ATT_HEADS))
        outs["cvs"].append(to_bm(cvs, CONV_K - 1))
        outs["pls"].append(to_bm(pls, POOL_HIST))

    st = {k: jnp.stack(v) for k, v in outs.items()}
    return (yp, ys.reshape(b_s, t_s, D_MODEL), st["kp"], st["vp"], st["lfp"], st["cvp"], st["plp"],
            st["ks"], st["vs"], st["lfs"], st["cvs"], st["pls"])
```

```python
import functools

import jax
import jax.numpy as jnp
from jax import lax
from jax.experimental import pallas as pl
from jax.experimental.pallas import tpu as pltpu

D_MODEL = 1024
ATT_HEADS = 8
HEAD_DIM = 64
ATT_WIDTH = ATT_HEADS * HEAD_DIM
CONV_WIDTH = 256
CONV_K = 3
POOL_WINDOWS = (2, 4, 8, 16)
POOL_WIDTH = 256
POOL_GROUP_DIM = POOL_WIDTH // len(POOL_WINDOWS)
POOL_HIST = max(POOL_WINDOWS) - 1
D_FF = 2816
PLE_DIM = 256
PAGE_SIZE = 128
RMS_EPS = 1e-6
Q_SCALE = HEAD_DIM ** -0.5

LANES = 128
SUBLANES = 8
VMEM_LIMIT_BYTES = 56 * 1024 * 1024

RQ, RK, RV = 0, ATT_WIDTH, 2 * ATT_WIDTH
RCB = 3 * ATT_WIDTH
RCC = RCB + CONV_WIDTH
RCH = RCC + CONV_WIDTH
RPU = RCH + CONV_WIDTH
RF = RPU + POOL_WIDTH
R_WIDTH = RF + LANES
F_REP = 6
F_LANES = F_REP * ATT_HEADS

NEG = float(jnp.finfo(jnp.float32).min)
BF16 = jnp.bfloat16
F32 = jnp.float32


def _rms(x, g):
    return x * lax.rsqrt(jnp.mean(x * x, axis=-1, keepdims=True) + RMS_EPS) * g


def _log_sigmoid(t):
    return jnp.minimum(t, 0.0) - jnp.log1p(jnp.exp(-jnp.abs(t)))


def _split3(c):
    hi = c.astype(BF16).astype(F32)
    r1 = c - hi
    mid = r1.astype(BF16).astype(F32)
    lo = (r1 - mid).astype(BF16).astype(F32)
    return hi, mid, lo


def _dot(a, b):
    return jnp.dot(a, b, preferred_element_type=F32)


def _dot_nt(a, b):
    return lax.dot_general(a, b, (((1,), (1,)), ((), ())), preferred_element_type=F32)


def _pool_window_select(s2, s4, s8, s16, lane):
    return jnp.where(lane < POOL_GROUP_DIM, s2,
                     jnp.where(lane < 2 * POOL_GROUP_DIM, s4,
                               jnp.where(lane < 3 * POOL_GROUP_DIM, s8, s16)))


def _pool_window_width(lane):
    return jnp.where(lane < POOL_GROUP_DIM, POOL_WINDOWS[0],
                     jnp.where(lane < 2 * POOL_GROUP_DIM, POOL_WINDOWS[1],
                               jnp.where(lane < 3 * POOL_GROUP_DIM, POOL_WINDOWS[2], POOL_WINDOWS[3])))


def _inproj_prompt_kernel(x_ref, gpre_ref, w_ref, wvt_ref, bf_ref, convw_ref, poolw_ref, pscale_ref, gconv_ref,
                          gpool_ref,
                          q_ref, k16_ref, vt16_ref, aq_ref, ak_ref, k32_ref, vt32_ref, logf_ref, cp_ref,
                          convn_ref, pooln_ref,
                          carry_ref, gbuf_ref, ubuf_ref, *, ts):
    i = pl.program_id(1)

    @pl.when(i == 0)
    def _():
        carry_ref[...] = jnp.zeros_like(carry_ref)
        gbuf_ref[0:SUBLANES, :] = jnp.zeros((SUBLANES, CONV_WIDTH), F32)
        ubuf_ref[0:2 * SUBLANES, :] = jnp.zeros((2 * SUBLANES, POOL_WIDTH), F32)

    x = x_ref[...]
    h = _rms(x, gpre_ref[...]).astype(BF16)
    zqk = _dot(h, w_ref[:, RQ:RV])
    zr = _dot(h, w_ref[:, RCB:R_WIDTH])
    z = lambda lo, hi: zr[:, lo - RCB:hi - RCB]
    vt = _dot_nt(wvt_ref[...], h)

    zk = zqk[:, RK:RV]
    q_ref[...] = (zqk[:, RQ:RK] * Q_SCALE).astype(BF16)
    k16_ref[...] = zk.astype(BF16)
    k32_ref[...] = zk
    vt16_ref[...] = vt.astype(BF16)
    vt32_ref[...] = vt

    lf = _log_sigmoid(z(RF, R_WIDTH) + bf_ref[...])
    logf_ref[...] = lf[:, 0:ATT_HEADS]
    row = lax.broadcasted_iota(jnp.int32, (ts, LANES), 0)
    lane = lax.broadcasted_iota(jnp.int32, (ts, LANES), 1)
    c = lf
    s = 1
    while s < ts:
        c = c + jnp.where(row >= s, pltpu.roll(c, s, 0), 0.0)
        s *= 2
    c = c + carry_ref[...]
    carry_ref[...] = c[ts - 1:ts, :]
    hi, mid, lo = _split3(c)
    aq = jnp.where(lane < 8, hi, jnp.where(lane < 16, mid, jnp.where(lane < 24, lo,
                                                                     jnp.where(lane < F_LANES, 1.0, 0.0))))
    ak = jnp.where(lane < 24, 1.0, jnp.where(lane < 32, -hi, jnp.where(lane < 40, -mid,
                                                                       jnp.where(lane < F_LANES, -lo, 0.0))))
    aq_ref[...] = aq.astype(BF16)
    ak_ref[...] = ak.astype(BF16)

    g = z(RCC, RCH) * z(RCH, RPU)
    gbuf_ref[SUBLANES:SUBLANES + ts, :] = g
    cw = convw_ref[...]
    conv = (cw[0:1, :] * gbuf_ref[SUBLANES - 2:SUBLANES - 2 + ts, :]
            + cw[1:2, :] * gbuf_ref[SUBLANES - 1:SUBLANES - 1 + ts, :]
            + cw[2:3, :] * g)
    conv = z(RCB, RCC) * conv
    convn_ref[...] = gbuf_ref[SUBLANES + ts - (CONV_K - 1):SUBLANES + ts, :]
    gbuf_ref[0:SUBLANES, :] = gbuf_ref[ts:ts + SUBLANES, :]

    u = z(RPU, RF)
    hist = 2 * SUBLANES
    ubuf_ref[hist:hist + ts, :] = u
    f0 = ubuf_ref[...]
    s2 = f0 + pltpu.roll(f0, 1, 0)
    s4 = s2 + pltpu.roll(s2, 2, 0)
    s8 = s4 + pltpu.roll(s4, 4, 0)
    s16 = s8 + pltpu.roll(s8, 8, 0)
    lane_p = lax.broadcasted_iota(jnp.int32, (ts, POOL_WIDTH), 1)
    row_p = lax.broadcasted_iota(jnp.int32, (ts, POOL_WIDTH), 0)
    win = _pool_window_select(s2[hist:], s4[hist:], s8[hist:], s16[hist:], lane_p)
    cnt = jnp.minimum(i * ts + row_p + 1, _pool_window_width(lane_p)).astype(F32)
    diff = win / cnt - u
    pool = _dot(diff.astype(BF16), poolw_ref[...]) * pscale_ref[...]
    pooln_ref[...] = ubuf_ref[hist + ts - POOL_HIST:hist + ts, :]
    ubuf_ref[0:hist, :] = ubuf_ref[ts:ts + hist, :]

    cp_ref[:, 0:CONV_WIDTH] = _rms(conv, gconv_ref[...]).astype(BF16)
    cp_ref[:, CONV_WIDTH:CONV_WIDTH + POOL_WIDTH] = _rms(pool, gpool_ref[...]).astype(BF16)


def _inproj_prompt(x, gpre, w_r, wvt, bf_r, convw, poolw, pscale, gconv, gpool, *, ts):
    b, s, d = x.shape
    n = s // ts
    row_spec = lambda w: pl.BlockSpec((None, ts, w), lambda bi, i: (bi, i, 0))
    col_spec = pl.BlockSpec((None, ATT_WIDTH, ts), lambda bi, i: (bi, 0, i))
    full = lambda a: pl.BlockSpec(a.shape, lambda bi, i: (0,) * a.ndim)
    out_shape = (
        jax.ShapeDtypeStruct((b, s, ATT_WIDTH), BF16),
        jax.ShapeDtypeStruct((b, s, ATT_WIDTH), BF16),
        jax.ShapeDtypeStruct((b, ATT_WIDTH, s), BF16),
        jax.ShapeDtypeStruct((b, s, LANES), BF16),
        jax.ShapeDtypeStruct((b, s, LANES), BF16),
        jax.ShapeDtypeStruct((b, s, ATT_WIDTH), F32),
        jax.ShapeDtypeStruct((b, ATT_WIDTH, s), F32),
        jax.ShapeDtypeStruct((b, s, ATT_HEADS), F32),
        jax.ShapeDtypeStruct((b, s, CONV_WIDTH + POOL_WIDTH), BF16),
        jax.ShapeDtypeStruct((b, CONV_K - 1, CONV_WIDTH), F32),
        jax.ShapeDtypeStruct((b, POOL_HIST, POOL_WIDTH), F32),
    )
    out_specs = (
        row_spec(ATT_WIDTH), row_spec(ATT_WIDTH), col_spec, row_spec(LANES), row_spec(LANES),
        row_spec(ATT_WIDTH), col_spec, row_spec(ATT_HEADS), row_spec(CONV_WIDTH + POOL_WIDTH),
        pl.BlockSpec((None, CONV_K - 1, CONV_WIDTH), lambda bi, i: (bi, 0, 0)),
        pl.BlockSpec((None, POOL_HIST, POOL_WIDTH), lambda bi, i: (bi, 0, 0)),
    )
    return pl.pallas_call(
        functools.partial(_inproj_prompt_kernel, ts=ts),
        grid=(b, n),
        in_specs=[row_spec(d), full(gpre), full(w_r), full(wvt), full(bf_r), full(convw), full(poolw),
                  full(pscale), full(gconv), full(gpool)],
        out_specs=out_specs,
        out_shape=out_shape,
        scratch_shapes=[pltpu.VMEM((1, LANES), F32),
                        pltpu.VMEM((ts + SUBLANES, CONV_WIDTH), F32),
                        pltpu.VMEM((ts + 2 * SUBLANES, POOL_WIDTH), F32)],
        compiler_params=pltpu.CompilerParams(dimension_semantics=("parallel", "arbitrary"),
                                             vmem_limit_bytes=VMEM_LIMIT_BYTES),
        name="inproj_prompt",
    )(x, gpre, w_r, wvt, bf_r, convw, poolw, pscale, gconv, gpool)


def _attn_prompt_kernel(q_ref, aq_ref, k_ref, ak_ref, vt_ref, o_ref, *, blk):
    pair = pl.program_id(1)
    s_len = q_ref.shape[0]
    n_blk = s_len // blk
    lane = lax.broadcasted_iota(jnp.int32, (blk, LANES), 1)
    key_id = lax.broadcasted_iota(jnp.int32, (blk, 2 * blk), 0)
    col_id = lax.broadcasted_iota(jnp.int32, (blk, 2 * blk), 1)
    causal = key_id <= jnp.where(col_id >= blk, col_id - blk, col_id)

    def q_block(qi, _):
        q0 = pl.multiple_of(qi * blk, blk)
        qd = q_ref[pl.ds(q0, blk), :]
        qa = aq_ref[pl.ds(q0, blk), :]
        halves = []
        for e in range(2):
            in_half = (lane >= e * HEAD_DIM) & (lane < (e + 1) * HEAD_DIM)
            head = 2 * pair + e
            halves.append(jnp.concatenate(
                [jnp.where(in_half, qd, jnp.zeros_like(qd)),
                 jnp.where(((lane & (ATT_HEADS - 1)) == head) & (lane < F_LANES), qa, jnp.zeros_like(qa))],
                axis=1))
        qq = jnp.concatenate(halves, axis=0)

        def step(kv, carry, masked):
            m, l, acc = carry
            k0 = pl.multiple_of(kv * blk, blk)
            kk = jnp.concatenate([k_ref[pl.ds(k0, blk), :], ak_ref[pl.ds(k0, blk), :]], axis=1)
            st = _dot_nt(kk, qq)
            if masked:
                st = jnp.where(causal, st, NEG)
            m_new = jnp.maximum(m, jnp.max(st, axis=0, keepdims=True))
            alpha = jnp.exp(m - m_new)
            p = jnp.exp(st - m_new)
            l = alpha * l + jnp.sum(p, axis=0, keepdims=True)
            acc = alpha * acc + _dot(vt_ref[:, pl.ds(k0, blk)], p.astype(BF16))
            return m_new, l, acc

        init = (jnp.full((1, 2 * blk), NEG, F32), jnp.zeros((1, 2 * blk), F32),
                jnp.zeros((LANES, 2 * blk), F32))
        carry = lax.fori_loop(0, qi, lambda kv, c: step(kv, c, False), init)
        m, l, acc = step(qi, carry, True)
        out = acc / l
        out_t = jnp.concatenate([out[0:HEAD_DIM, 0:blk], out[HEAD_DIM:LANES, blk:2 * blk]], axis=0)
        o_ref[pl.ds(q0, blk), :] = out_t.T
        return 0

    lax.fori_loop(0, n_blk, q_block, 0)


def _attn_prompt(q, aq, k16, ak, vt16, *, blk):
    b, s, _ = q.shape
    pairs = ATT_WIDTH // LANES
    pair_spec = pl.BlockSpec((None, s, LANES), lambda bi, j: (bi, 0, j))
    aug_spec = pl.BlockSpec((None, s, LANES), lambda bi, j: (bi, 0, 0))
    vt_spec = pl.BlockSpec((None, LANES, s), lambda bi, j: (bi, j, 0))
    return pl.pallas_call(
        functools.partial(_attn_prompt_kernel, blk=blk),
        grid=(b, pairs),
        in_specs=[pair_spec, aug_spec, pair_spec, aug_spec, vt_spec],
        out_specs=pair_spec,
        out_shape=jax.ShapeDtypeStruct((b, s, ATT_WIDTH), F32),
        compiler_params=pltpu.CompilerParams(dimension_semantics=("parallel", "parallel"),
                                             vmem_limit_bytes=VMEM_LIMIT_BYTES),
        name="attn_prompt",
    )(q, aq, k16, ak, vt16)


FF_CHUNK = 1408


def _outffn_kernel(x_ref, att_ref, cp_ref, ple_ref, gatt_ref, wo_ref, gpm_ref, gpf_ref, wgu_ref, wd_ref,
                   gpo_ref, wpp_ref, wpg_ref, y_ref):
    att_n = _rms(att_ref[...], gatt_ref[...]).astype(BF16)
    o = _dot(att_n, wo_ref[0:ATT_WIDTH, :]) + _dot(cp_ref[...], wo_ref[ATT_WIDTH:, :])
    x1 = x_ref[...] + _rms(o, gpm_ref[...])
    h2 = _rms(x1, gpf_ref[...]).astype(BF16)
    f = None
    for c0 in range(0, D_FF, FF_CHUNK):
        gate = _dot(h2, wgu_ref[:, c0:c0 + FF_CHUNK])
        up = _dot(h2, wgu_ref[:, D_FF + c0:D_FF + c0 + FF_CHUNK])
        part = _dot((gate * jax.nn.sigmoid(gate) * up).astype(BF16), wd_ref[c0:c0 + FF_CHUNK, :])
        f = part if f is None else f + part
    x2 = x1 + _rms(f, gpo_ref[...])
    emb = _dot(ple_ref[...].astype(BF16), wpp_ref[...])
    y_ref[...] = x2 + emb * jax.nn.sigmoid(_dot(x2.astype(BF16), wpg_ref[...]))


def _outffn(x, att, cp, ple, gatt, wo, gpm, gpf, wgu, wd, gpo, wpp, wpg, *, tm):
    t, d = x.shape
    row_spec = lambda w: pl.BlockSpec((tm, w), lambda i: (i, 0))
    full = lambda a: pl.BlockSpec(a.shape, lambda i: (0,) * a.ndim, pipeline_mode=pl.Buffered(1))
    return pl.pallas_call(
        _outffn_kernel,
        grid=(t // tm,),
        in_specs=[row_spec(d), row_spec(ATT_WIDTH), row_spec(CONV_WIDTH + POOL_WIDTH), row_spec(PLE_DIM),
                  full(gatt), full(wo), full(gpm), full(gpf), full(wgu), full(wd), full(gpo), full(wpp),
                  full(wpg)],
        out_specs=row_spec(d),
        out_shape=jax.ShapeDtypeStruct((t, d), F32),
        compiler_params=pltpu.CompilerParams(dimension_semantics=("parallel",),
                                             vmem_limit_bytes=VMEM_LIMIT_BYTES),
        name="outffn",
    )(x, att, cp, ple, gatt, wo, gpm, gpf, wgu, wd, gpo, wpp, wpg)


def _inproj_sample_kernel(xbm_ref, xtm_ref, gpre_ref, w_ref, wt_ref, bf_ref, bfcol_ref, convw_ref, poolw_ref,
                          pscale_ref, gconv_ref, gpool_ref, chist_ref, phist_ref,
                          q_ref, k32_ref, v32_ref, logf_ref, kt_ref, vt_ref, cumt_ref, cp_ref, convn_ref,
                          pooln_ref, *, n_b, n_t, pos0):
    rows = n_b * n_t
    h_bm = _rms(xbm_ref[...], gpre_ref[...]).astype(BF16)
    h_tm = _rms(xtm_ref[...], gpre_ref[...]).astype(BF16)

    z = _dot(h_bm, w_ref[:, RQ:RCB])
    q_ref[...] = z[:, RQ:RK] * Q_SCALE
    k32_ref[...] = z[:, RK:RV]
    v32_ref[...] = z[:, RV:RCB]
    zf = _dot(h_bm, w_ref[:, RF:R_WIDTH])
    logf_ref[...] = _log_sigmoid(zf + bf_ref[...])[:, 0:ATT_HEADS]

    zt = _dot_nt(wt_ref[...], h_bm)
    kt_ref[...] = zt[0:ATT_WIDTH, :].astype(BF16)
    vt_ref[...] = zt[ATT_WIDTH:2 * ATT_WIDTH, :].astype(BF16)
    lft = _log_sigmoid(zt[2 * ATT_WIDTH:, :] + bfcol_ref[...])
    lane = lax.broadcasted_iota(jnp.int32, lft.shape, 1)
    t_of = lane % n_t
    c = lft
    s = 1
    while s < n_t:
        c = c + jnp.where(t_of >= s, pltpu.roll(c, s, 1), 0.0)
        s *= 2
    cumt_ref[...] = c

    zc = _dot(h_tm, w_ref[:, RCB:RF])
    g = zc[:, CONV_WIDTH:2 * CONV_WIDTH] * zc[:, 2 * CONV_WIDTH:3 * CONV_WIDTH]
    full_g = jnp.concatenate([chist_ref[...], g], axis=0)
    cw = convw_ref[...]
    conv = sum(cw[j:j + 1, :] * full_g[j * n_b:j * n_b + rows, :] for j in range(CONV_K))
    conv = zc[:, 0:CONV_WIDTH] * conv
    convn_ref[...] = full_g[rows:, :]

    u = zc[:, 3 * CONV_WIDTH:]
    full_u = jnp.concatenate([phist_ref[...], u], axis=0)
    pooln_ref[...] = full_u[rows:, :]
    sums = []
    cur, off = full_u, 0
    for w in POOL_WINDOWS:
        sh = (w // 2) * n_b
        cur = cur[sh:, :] + cur[:cur.shape[0] - sh, :]
        off += sh
        sums.append(cur[POOL_HIST * n_b - off:POOL_HIST * n_b - off + rows, :])
    lane_p = lax.broadcasted_iota(jnp.int32, (rows, POOL_WIDTH), 1)
    row_p = lax.broadcasted_iota(jnp.int32, (rows, POOL_WIDTH), 0)
    win = _pool_window_select(*sums, lane_p)
    cnt = jnp.minimum(pos0 + row_p // n_b + 1, _pool_window_width(lane_p)).astype(F32)
    diff = win / cnt - u
    pool = _dot(diff.astype(BF16), poolw_ref[...]) * pscale_ref[...]

    cp_ref[:, 0:CONV_WIDTH] = _rms(conv, gconv_ref[...]).astype(BF16)
    cp_ref[:, CONV_WIDTH:CONV_WIDTH + POOL_WIDTH] = _rms(pool, gpool_ref[...]).astype(BF16)


def _inproj_sample(xbm, xtm, gpre, w_r, wt, bf_r, bfcol, convw, poolw, pscale, gconv, gpool, chist, phist,
                   *, n_b, n_t, pos0):
    rows = n_b * n_t
    out_shape = (
        jax.ShapeDtypeStruct((rows, ATT_WIDTH), F32),
        jax.ShapeDtypeStruct((rows, ATT_WIDTH), F32),
        jax.ShapeDtypeStruct((rows, ATT_WIDTH), F32),
        jax.ShapeDtypeStruct((rows, ATT_HEADS), F32),
        jax.ShapeDtypeStruct((ATT_WIDTH, rows), BF16),
        jax.ShapeDtypeStruct((ATT_WIDTH, rows), BF16),
        jax.ShapeDtypeStruct((ATT_HEADS, rows), F32),
        jax.ShapeDtypeStruct((rows, CONV_WIDTH + POOL_WIDTH), BF16),
        jax.ShapeDtypeStruct(((CONV_K - 1) * n_b, CONV_WIDTH), F32),
        jax.ShapeDtypeStruct((POOL_HIST * n_b, POOL_WIDTH), F32),
    )
    return pl.pallas_call(
        functools.partial(_inproj_sample_kernel, n_b=n_b, n_t=n_t, pos0=pos0),
        out_shape=out_shape,
        compiler_params=pltpu.CompilerParams(vmem_limit_bytes=VMEM_LIMIT_BYTES),
        name="inproj_sample",
    )(xbm, xtm, gpre, w_r, wt, bf_r, bfcol, convw, poolw, pscale, gconv, gpool, chist, phist)


PAGES_PER_GROUP = 8


def _attn_sample_kernel(pt_ref, q_ref, cumt_ref, ktn_ref, vtn_ref, kc_ref, vc_ref, lfc_ref, o_ref,
                        kbuf, vbuf, lfbuf, tail_ref, tot_ref, ksem, vsem, lfsem, *, layer, n_t, n_pages):
    b = pl.program_id(0)
    n_b = pl.num_programs(0)
    grp = PAGES_PER_GROUP
    n_groups = n_pages // grp
    rows = n_t * ATT_HEADS

    def lf_copy(p):
        return pltpu.make_async_copy(lfc_ref.at[layer, pt_ref[b, p]], lfbuf.at[p], lfsem)

    def kv_copies(bb, g, slot, i):
        page = pt_ref[bb, g * grp + i]
        return (pltpu.make_async_copy(kc_ref.at[layer, page], kbuf.at[slot, i], ksem.at[slot, i]),
                pltpu.make_async_copy(vc_ref.at[layer, page], vbuf.at[slot, i], vsem.at[slot, i]))

    def start_group(bb, g, slot):
        for i in range(grp):
            for cp in kv_copies(bb, g, slot, i):
                cp.start()

    def wait_group(g, slot):
        for i in range(grp):
            for cp in kv_copies(b, g, slot, i):
                cp.wait()

    def lf_start(p, _):
        lf_copy(p).start()
        return 0

    def lf_wait(p, _):
        lf_copy(p).wait()
        return 0

    lax.fori_loop(0, n_pages, lf_start, 0)

    @pl.when(b == 0)
    def _():
        start_group(0, 0, 0)

    lax.fori_loop(0, n_pages, lf_wait, 0)

    lp = lfbuf[...].reshape(n_pages * ATT_HEADS, PAGE_SIZE)
    r_in = lax.broadcasted_iota(jnp.int32, (PAGE_SIZE, 2 * PAGE_SIZE), 0)
    r_out = lax.broadcasted_iota(jnp.int32, (PAGE_SIZE, 2 * PAGE_SIZE), 1)
    after_or_all = ((r_in > r_out) | (r_out >= PAGE_SIZE)).astype(BF16)
    sums = sum(_dot(piece.astype(BF16), after_or_all) for piece in _split3(lp))
    tail_ref[...] = sums[:, 0:PAGE_SIZE].reshape(n_pages, ATT_HEADS, PAGE_SIZE)
    tot_ref[...] = sums[:, PAGE_SIZE:].reshape(n_pages, ATT_HEADS, PAGE_SIZE)

    def tail_step(k, later):
        p = n_pages - 1 - k
        tail_ref[p] = tail_ref[p] + later
        return later + tot_ref[p]

    lax.fori_loop(0, n_pages, tail_step, jnp.zeros((ATT_HEADS, PAGE_SIZE), F32))

    q_b = q_ref[...]
    sub = lax.broadcasted_iota(jnp.int32, (ATT_HEADS, ATT_WIDTH), 0)
    col_head = lax.broadcasted_iota(jnp.int32, (ATT_HEADS, ATT_WIDTH), 1) // HEAD_DIM
    diag = sub == col_head
    qbd = jnp.concatenate(
        [jnp.where(diag, jnp.broadcast_to(q_b[t:t + 1, :], (ATT_HEADS, ATT_WIDTH)), 0.0) for t in range(n_t)],
        axis=0).astype(BF16)
    cumt = cumt_ref[...]
    lane = lax.broadcasted_iota(jnp.int32, cumt.shape, 1)
    cq = jnp.concatenate(
        [jnp.sum(jnp.where(lane == b * n_t + t, cumt, 0.0), axis=1, keepdims=True) for t in range(n_t)],
        axis=0)
    cq_wide = jnp.broadcast_to(cq, (rows, PAGE_SIZE))

    def update(carry, scs, vts):
        m, l, acc = carry
        blk_max = functools.reduce(jnp.maximum, scs)
        m_new = jnp.maximum(m, jnp.max(blk_max, axis=-1, keepdims=True))
        alpha = jnp.exp(m - m_new)
        ps = [jnp.exp(sc - m_new) for sc in scs]
        l = alpha * l + jnp.sum(functools.reduce(jnp.add, ps), axis=-1, keepdims=True)
        pv = functools.reduce(jnp.add, [_dot_nt(p.astype(BF16), vt) for p, vt in zip(ps, vts)])
        return m_new, l, alpha * acc + pv

    def group_step(g, carry):
        slot = g % 2

        @pl.when(g + 1 < n_groups)
        def _():
            start_group(b, g + 1, 1 - slot)

        @pl.when((g + 1 == n_groups) & (b + 1 < n_b))
        def _():
            start_group(b + 1, 0, 1 - slot)

        wait_group(g, slot)
        scs, vts = [], []
        for i in range(grp):
            bias = cq_wide + jnp.concatenate([tail_ref[g * grp + i]] * n_t, axis=0)
            scs.append(_dot(qbd, kbuf[slot, i].astype(BF16)) + bias)
            vts.append(vbuf[slot, i].astype(BF16))
        return update(carry, scs, vts)

    init = (jnp.full((rows, 1), NEG, F32), jnp.zeros((rows, 1), F32), jnp.zeros((rows, ATT_WIDTH), F32))
    carry = lax.fori_loop(0, n_groups, group_step, init)

    sc = _dot(qbd, ktn_ref[...]) + cq - jnp.concatenate([cumt] * n_t, axis=0)
    lane_r = lax.broadcasted_iota(jnp.int32, sc.shape, 1)
    t_row = lax.broadcasted_iota(jnp.int32, sc.shape, 0) // ATT_HEADS
    valid = (lane_r >= b * n_t) & (lane_r <= b * n_t + t_row)
    m, l, acc = update(carry, [jnp.where(valid, sc, NEG)], [vtn_ref[...]])

    out = acc / l
    o_ref[...] = jnp.concatenate(
        [jnp.sum(jnp.where(diag, out[t * ATT_HEADS:(t + 1) * ATT_HEADS, :], 0.0), axis=0, keepdims=True)
         for t in range(n_t)], axis=0)


def _attn_sample(page_table, q, cumt, ktn, vtn, kc, vc, lfc, *, layer):
    n_b, n_pages = page_table.shape
    n_t = q.shape[1]
    grp = PAGES_PER_GROUP
    grid_spec = pltpu.PrefetchScalarGridSpec(
        num_scalar_prefetch=1,
        grid=(n_b,),
        in_specs=[
            pl.BlockSpec((None, n_t, ATT_WIDTH), lambda b, pt: (b, 0, 0)),
            pl.BlockSpec(cumt.shape, lambda b, pt: (0, 0)),
            pl.BlockSpec(ktn.shape, lambda b, pt: (0, 0)),
            pl.BlockSpec(vtn.shape, lambda b, pt: (0, 0)),
            pl.BlockSpec(memory_space=pl.ANY),
            pl.BlockSpec(memory_space=pl.ANY),
            pl.BlockSpec(memory_space=pl.ANY),
        ],
        out_specs=pl.BlockSpec((None, n_t, ATT_WIDTH), lambda b, pt: (b, 0, 0)),
        scratch_shapes=[
            pltpu.VMEM((2, grp, ATT_WIDTH, PAGE_SIZE), F32),
            pltpu.VMEM((2, grp, ATT_WIDTH, PAGE_SIZE), F32),
            pltpu.VMEM((n_pages, ATT_HEADS, PAGE_SIZE), F32),
            pltpu.VMEM((n_pages, ATT_HEADS, PAGE_SIZE), F32),
            pltpu.VMEM((n_pages, ATT_HEADS, PAGE_SIZE), F32),
            pltpu.SemaphoreType.DMA((2, grp)),
            pltpu.SemaphoreType.DMA((2, grp)),
            pltpu.SemaphoreType.DMA(()),
        ],
    )
    return pl.pallas_call(
        functools.partial(_attn_sample_kernel, layer=layer, n_t=n_t, n_pages=n_pages),
        grid_spec=grid_spec,
        out_shape=jax.ShapeDtypeStruct((n_b, n_t, ATT_WIDTH), F32),
        compiler_params=pltpu.CompilerParams(dimension_semantics=("arbitrary",),
                                             vmem_limit_bytes=VMEM_LIMIT_BYTES),
        name="attn_sample",
    )(page_table, q, cumt, ktn, vtn, kc, vc, lfc)


def _prep_layer(l, w_in, b_f, pool_w, w_o, w_gu, w_down, w_ple_proj, w_ple_gate):
    wi = w_in[l]
    q_off, k_off, v_off = 0, ATT_WIDTH, 2 * ATT_WIDTH
    f_off = 3 * ATT_WIDTH
    rest = f_off + ATT_HEADS
    wf = wi[:, f_off:rest]
    f_slab = jnp.concatenate([wf] * F_REP + [jnp.zeros((D_MODEL, LANES - F_LANES), F32)], axis=1)
    w_r = jnp.concatenate([wi[:, q_off:f_off], wi[:, rest:], f_slab], axis=1).astype(BF16)
    wt = jnp.concatenate([wi[:, k_off:f_off], wf], axis=1).T.astype(BF16)
    bf = b_f[l]
    bf_r = jnp.concatenate([bf] * F_REP + [jnp.zeros((LANES - F_LANES,), F32)])[None, :]
    bfcol = bf[:, None]
    pw = pool_w[l]
    n_g = len(POOL_WINDOWS)
    poolw = jnp.zeros((n_g, POOL_GROUP_DIM, n_g, POOL_GROUP_DIM), F32)
    poolw = poolw.at[jnp.arange(n_g), :, jnp.arange(n_g), :].set(pw).reshape(POOL_WIDTH, POOL_WIDTH)
    return dict(w_r=w_r, wt=wt, bf_r=bf_r, bfcol=bfcol, poolw=poolw.astype(BF16),
                wo=w_o[l].astype(BF16), wgu=w_gu[l].astype(BF16), wd=w_down[l].astype(BF16),
                wpp=w_ple_proj[l].astype(BF16), wpg=w_ple_gate[l].astype(BF16))


def kernel(x_prompt, x_sample, p_prompt, p_sample, cache_k, cache_v, cache_logf, state_conv, state_pool,
           page_table, g_pre_mix, w_in, b_f, conv_w, pool_w, pool_scale, g_att, g_conv, g_pool, w_o,
           g_post_mix, g_pre_ffn, w_gu, w_down, g_post_ffn, w_ple_proj, w_ple_gate):
    depth = w_in.shape[0]
    b_p, s_p, _ = x_prompt.shape
    b_s, t_s, _ = x_sample.shape
    n_pool = cache_k.shape[1]
    n_past = page_table.shape[1] * PAGE_SIZE
    rows_s = b_s * t_s

    kc = jnp.transpose(cache_k, (0, 1, 3, 4, 2)).reshape(depth, n_pool, ATT_WIDTH, PAGE_SIZE)
    vc = jnp.transpose(cache_v, (0, 1, 3, 4, 2)).reshape(depth, n_pool, ATT_WIDTH, PAGE_SIZE)
    lfc = jnp.transpose(cache_logf, (0, 1, 3, 2))

    def to_tm(a):
        return jnp.swapaxes(a, 0, 1).reshape(a.shape[0] * a.shape[1], a.shape[2])

    def to_bm(a, n_t):
        return jnp.swapaxes(a.reshape(n_t, b_s, a.shape[-1]), 0, 1)

    yp = x_prompt
    ys = x_sample.reshape(rows_s, D_MODEL)
    outs = {k: [] for k in ("kp", "vp", "lfp", "cvp", "plp", "ks", "vs", "lfs", "cvs", "pls")}
    row2 = lambda a: a[None, :]
    for l in range(depth):
        w = _prep_layer(l, w_in, b_f, pool_w, w_o, w_gu, w_down, w_ple_proj, w_ple_gate)
        gains = (row2(g_att[l]), w["wo"], row2(g_post_mix[l]), row2(g_pre_ffn[l]), w["wgu"], w["wd"],
                 row2(g_post_ffn[l]), w["wpp"], w["wpg"])
        mix_w = (conv_w[l], w["poolw"], row2(pool_scale[l]), row2(g_conv[l]), row2(g_pool[l]))

        q, k16, vt16, aq, ak, k32, vt32, lf, cp, cvn, pln = _inproj_prompt(
            yp, row2(g_pre_mix[l]), w["w_r"], w["wt"][ATT_WIDTH:2 * ATT_WIDTH], w["bf_r"], *mix_w, ts=512)
        att = _attn_prompt(q, aq, k16, ak, vt16, blk=512)
        yp = _outffn(yp.reshape(b_p * s_p, D_MODEL), att.reshape(b_p * s_p, ATT_WIDTH),
                     cp.reshape(b_p * s_p, -1), p_prompt[l].reshape(b_p * s_p, PLE_DIM), *gains,
                     tm=512).reshape(b_p, s_p, D_MODEL)
        outs["kp"].append(k32.reshape(b_p, s_p, ATT_HEADS, HEAD_DIM))
        outs["vp"].append(jnp.transpose(vt32.reshape(b_p, ATT_HEADS, HEAD_DIM, s_p), (0, 3, 1, 2)))
        outs["lfp"].append(lf)
        outs["cvp"].append(cvn)
        outs["plp"].append(pln)

        xtm = to_tm(ys.reshape(b_s, t_s, D_MODEL))
        qs, ks32, vs32, lfs, ktn, vtn, cumt, cps, cvs, pls = _inproj_sample(
            ys, xtm, row2(g_pre_mix[l]), w["w_r"], w["wt"], w["bf_r"], w["bfcol"], *mix_w,
            to_tm(state_conv[l]), to_tm(state_pool[l]), n_b=b_s, n_t=t_s, pos0=n_past)
        att_s = _attn_sample(page_table, qs.reshape(b_s, t_s, ATT_WIDTH), cumt, ktn, vtn, kc, vc, lfc,
                             layer=l)
        ys = _outffn(ys, att_s.reshape(rows_s, ATT_WIDTH), to_bm(cps, t_s).reshape(rows_s, -1),
                     p_sample[l].reshape(rows_s, PLE_DIM), *gains, tm=rows_s)
        outs["ks"].append(ks32.reshape(b_s, t_s, ATT_HEADS, HEAD_DIM))
        outs["vs"].append(vs32.reshape(b_s, t_s, ATT_HEADS, HEAD_DIM))
        outs["lfs"].append(lfs.reshape(b_s, t_s, ATT_HEADS))
        outs["cvs"].append(to_bm(cvs, CONV_K - 1))
        outs["pls"].append(to_bm(pls, POOL_HIST))

    st = {k: jnp.stack(v) for k, v in outs.items()}
    return (yp, ys.reshape(b_s, t_s, D_MODEL), st["kp"], st["vp"], st["lfp"], st["cvp"], st["plp"],
            st["ks"], st["vs"], st["lfs"], st["cvs"], st["pls"])
```
